```python
import jax, jax.numpy as jnp
from jax import lax
import numpy as np

D_MODEL = 2048
BATCH = 1
SEQ = 16384
DEPTH = 4
DEC_BATCH = 32
DEC_SEQ = 16
PAST_LEN = 1024

CHUNK = 64
N_POOL = (DEPTH + 1) // 2
N_SGU = DEPTH // 2
POOL_WINDOWS = (2, 4, 8, 16)
POOL_GROUPS = len(POOL_WINDOWS)
POOL_GROUP_DIM = D_MODEL // POOL_GROUPS
POOL_HIST = max(POOL_WINDOWS) - 1
SGU_CHUNK = 128
D_SGU = 2 * D_MODEL
SGU_HEADS = 8
SGU_HEAD_DIM = D_SGU // SGU_HEADS
D_FF = 5632
N_EXPERTS = 8
MOE_TOP_K = 2
D_EXP = 7168
MOE_BLOCK = 256
EPS = 1e-6

kernel_name = "pool_sgu_moe_streaming_encoder_step"


def rms_norm(x, g):
    xf = x.astype(jnp.float32)
    y = xf * lax.rsqrt(jnp.mean(xf * xf, axis=-1, keepdims=True) + EPS) * g.astype(jnp.float32)
    return y.astype(x.dtype)


def pool_mixer(h, hist, pos0, w_grp, scale):
    B, T, D = h.shape
    P = POOL_HIST
    hext = jnp.concatenate([hist, h], axis=1).astype(jnp.float32)
    csum = jnp.concatenate([jnp.zeros((B, 1, D), jnp.float32), jnp.cumsum(hext, axis=1)], axis=1)
    end = csum[:, P + 1:P + 1 + T]
    pos = pos0 + jnp.arange(T)
    outs = []
    for g, w in enumerate(POOL_WINDOWS):
        sl = slice(g * POOL_GROUP_DIM, (g + 1) * POOL_GROUP_DIM)
        start = csum[:, P + 1 - w:P + 1 - w + T, sl]
        cnt = jnp.minimum(pos + 1, w).astype(jnp.float32)[None, :, None]
        outs.append((end[..., sl] - start) / cnt)
    pooled = (jnp.concatenate(outs, axis=-1) - hext[:, P:]).astype(h.dtype)
    pooled = pooled.reshape(B, T, POOL_GROUPS, POOL_GROUP_DIM)
    y = jnp.einsum('btgc,gcd->btgd', pooled, w_grp).reshape(B, T, D)
    return y * scale, hext[:, -P:].astype(h.dtype)


def sgu_mixer(h, w_in, v_gain, w_s, b_s, w_out):
    B, T, _ = h.shape
    z = jax.nn.gelu(h @ w_in, approximate=False)
    u, v = jnp.split(z, 2, axis=-1)
    v = rms_norm(v, v_gain)
    L = SGU_CHUNK if T >= SGU_CHUNK else T
    n = T // L
    mask = jnp.tril(jnp.ones((L, L), dtype=bool))
    ws = jnp.where(mask[None], w_s[:, :L, :L], 0)
    vc = v.reshape(B, n, L, SGU_HEADS, SGU_HEAD_DIM)
    mixed = jnp.einsum('gij,bnjgc->bnigc', ws, vc) + b_s[:, :L].T[None, None, :, :, None]
    s = u * mixed.reshape(B, T, D_SGU)
    return s @ w_out, v


def swiglu(h, w1, w3, w2):
    return (jax.nn.silu(h @ w1) * (h @ w3)) @ w2


def moe_swiglu(h, w_router, b_router, w1, w3, w2):
    shp = h.shape
    hf = h.reshape(-1, shp[-1])
    n = hf.shape[0]
    nk = n * MOE_TOP_K
    logits = hf.astype(jnp.float32) @ w_router.astype(jnp.float32) + b_router.astype(jnp.float32)
    top_val, top_idx = lax.top_k(logits, MOE_TOP_K)
    gates = jax.nn.softmax(top_val, axis=-1).astype(h.dtype)
    e_flat = top_idx.reshape(-1)
    order = jnp.argsort(e_flat)
    sorted_e = e_flat[order]
    counts = jnp.bincount(e_flat, length=N_EXPERTS)
    padded = (counts + MOE_BLOCK - 1) // MOE_BLOCK * MOE_BLOCK
    pad_end = jnp.cumsum(padded)
    pad_start = pad_end - padded
    grp_start = jnp.cumsum(counts) - counts
    rank = jnp.arange(nk, dtype=jnp.int32) - grp_start[sorted_e]
    dest = jnp.zeros((nk,), jnp.int32).at[order].set((pad_start[sorted_e] + rank).astype(jnp.int32))
    n_blocks = -(-nk // MOE_BLOCK) + N_EXPERTS
    rows = n_blocks * MOE_BLOCK
    row_tok = jnp.zeros((rows,), jnp.int32).at[dest].set(jnp.arange(nk, dtype=jnp.int32) // MOE_TOP_K)
    block_expert = jnp.minimum(
        jnp.searchsorted(pad_end, jnp.arange(n_blocks) * MOE_BLOCK, side='right'), N_EXPERTS - 1)

    def expert_block(args):
        tok, e = args
        xb = hf[tok]
        return (jax.nn.silu(xb @ w1[e]) * (xb @ w3[e])) @ w2[e]

    out = lax.map(expert_block, (row_tok.reshape(n_blocks, MOE_BLOCK), block_expert)).reshape(rows, -1)
    y = jnp.einsum('nk,nkd->nd', gates, out[dest.reshape(n, MOE_TOP_K)])
    return y.reshape(shp)


def trunk(x, pool_hist, pos0, norm_mix, norm_ffn, norm_final, pool_w, pool_scale,
          sgu_w_in, sgu_v_gain, sgu_w_s, sgu_b_s, sgu_w_out,
          ffn_w1, ffn_w3, ffn_w2, moe_router, moe_router_b, moe_w1, moe_w3, moe_w2):
    pool_states, sgu_states = [], []
    for i in range(DEPTH):
        j = i // 2
        h = rms_norm(x, norm_mix[i])
        if i % 2 == 0:
            y, st = pool_mixer(h, pool_hist[j], pos0, pool_w[j], pool_scale[j])
            pool_states.append(st)
        else:
            y, st = sgu_mixer(h, sgu_w_in[j], sgu_v_gain[j], sgu_w_s[j], sgu_b_s[j], sgu_w_out[j])
            sgu_states.append(st)
        x = x + y
        h = rms_norm(x, norm_ffn[i])
        if i % 2 == 0:
            x = x + swiglu(h, ffn_w1[j], ffn_w3[j], ffn_w2[j])
        else:
            x = x + moe_swiglu(h, moe_router[j], moe_router_b[j], moe_w1[j], moe_w3[j], moe_w2[j])
    return rms_norm(x, norm_final), jnp.stack(pool_states), jnp.stack(sgu_states)


def setup_inputs(seed: int = 0) -> dict:
    key = jax.random.key(seed)
    ks = jax.random.split(key, 21)
    f32 = jnp.float32

    def nrm(k, shape, scale):
        return jax.random.normal(k, shape, f32) * scale

    return {
        "x_prompt": nrm(ks[0], (BATCH, SEQ, D_MODEL), 1.0),
        "x_sample": nrm(ks[1], (DEC_BATCH, DEC_SEQ, D_MODEL), 1.0),
        "state_pool": nrm(ks[2], (N_POOL, DEC_BATCH, POOL_HIST, D_MODEL), 1.0),
        "norm_mix": 1.0 + nrm(ks[3], (DEPTH, D_MODEL), 0.05),
        "norm_ffn": 1.0 + nrm(ks[4], (DEPTH, D_MODEL), 0.05),
        "norm_final": 1.0 + nrm(ks[5], (D_MODEL,), 0.05),
        "pool_w": nrm(ks[6], (N_POOL, POOL_GROUPS, POOL_GROUP_DIM, POOL_GROUP_DIM), POOL_GROUP_DIM ** -0.5),
        "pool_scale": 1.0 + nrm(ks[7], (N_POOL, D_MODEL), 0.1),
        "sgu_w_in": nrm(ks[8], (N_SGU, D_MODEL, 2 * D_SGU), D_MODEL ** -0.5),
        "sgu_v_gain": 1.0 + nrm(ks[9], (N_SGU, D_SGU), 0.05),
        "sgu_w_s": nrm(ks[10], (N_SGU, SGU_HEADS, SGU_CHUNK, SGU_CHUNK), SGU_CHUNK ** -0.5),
        "sgu_b_s": 1.0 + nrm(ks[11], (N_SGU, SGU_HEADS, SGU_CHUNK), 0.1),
        "sgu_w_out": nrm(ks[12], (N_SGU, D_SGU, D_MODEL), D_SGU ** -0.5),
        "ffn_w1": nrm(ks[13], (N_POOL, D_MODEL, D_FF), D_MODEL ** -0.5),
        "ffn_w3": nrm(ks[14], (N_POOL, D_MODEL, D_FF), D_MODEL ** -0.5),
        "ffn_w2": nrm(ks[15], (N_POOL, D_FF, D_MODEL), D_FF ** -0.5),
        "moe_router": nrm(ks[16], (N_SGU, D_MODEL, N_EXPERTS), D_MODEL ** -0.5),
        "moe_router_b": nrm(ks[17], (N_SGU, N_EXPERTS), 0.01),
        "moe_w1": nrm(ks[18], (N_SGU, N_EXPERTS, D_MODEL, D_EXP), D_MODEL ** -0.5),
        "moe_w3": nrm(ks[19], (N_SGU, N_EXPERTS, D_MODEL, D_EXP), D_MODEL ** -0.5),
        "moe_w2": nrm(ks[20], (N_SGU, N_EXPERTS, D_EXP, D_MODEL), D_EXP ** -0.5),
    }


def reference(x_prompt, x_sample, state_pool, norm_mix, norm_ffn, norm_final, pool_w, pool_scale,
              sgu_w_in, sgu_v_gain, sgu_w_s, sgu_b_s, sgu_w_out, ffn_w1, ffn_w3, ffn_w2,
              moe_router, moe_router_b, moe_w1, moe_w3, moe_w2):
    weights = (norm_mix, norm_ffn, norm_final, pool_w, pool_scale,
               sgu_w_in, sgu_v_gain, sgu_w_s, sgu_b_s, sgu_w_out,
               ffn_w1, ffn_w3, ffn_w2, moe_router, moe_router_b, moe_w1, moe_w3, moe_w2)
    prompt_hist = jnp.zeros((N_POOL, x_prompt.shape[0], POOL_HIST, D_MODEL), x_prompt.dtype)
    y_prompt, new_pool_prompt, _ = trunk(x_prompt, prompt_hist, 0, *weights)
    y_sample, new_pool_sample, new_sgu_v_sample = trunk(x_sample, state_pool, PAST_LEN, *weights)
    return (y_prompt, y_sample, new_pool_prompt, new_pool_sample, new_sgu_v_sample)
```

```python
import functools

import jax
import jax.numpy as jnp
from jax import lax
from jax.experimental import pallas as pl
from jax.experimental.pallas import tpu as pltpu

D_MODEL = 2048
SEQ = 16384
DEPTH = 4
DEC_BATCH = 32
DEC_SEQ = 16
PAST_LEN = 1024
POOL_WINDOWS = (2, 4, 8, 16)
POOL_GROUPS = len(POOL_WINDOWS)
POOL_GROUP_DIM = D_MODEL // POOL_GROUPS
POOL_HIST = max(POOL_WINDOWS) - 1
SGU_CHUNK = 128
D_SGU = 2 * D_MODEL
SGU_HEADS = 8
SGU_HEAD_DIM = D_SGU // SGU_HEADS
D_FF = 5632
N_EXPERTS = 8
D_EXP = 7168
EPS = 1e-6

N_SAMPLE = DEC_BATCH * DEC_SEQ
N_TOK = SEQ + N_SAMPLE
N_PAD = 17408
HIST_ROWS = POOL_HIST + 1

TM_SMALL = 256
N_SMALL = N_PAD // TM_SMALL
N_SMALL_PROMPT = SEQ // TM_SMALL
N_SMALL_TOK = N_TOK // TM_SMALL
SEQ_PER_TILE = TM_SMALL // DEC_SEQ

TM_BIG = 1024
N_BIG = N_PAD // TM_BIG
TF_FFN = 512
TM_SGU2 = 512
N_SGU2 = N_PAD // TM_SGU2
SAMPLE_TILE_SGU2 = SEQ // TM_SGU2
SAMPLE_TILE_BIG = SEQ // TM_BIG

TM_EXP = 1024
N_PAIRS = 2 * N_TOK
NB_EXP = -(-N_PAIRS // TM_EXP) + N_EXPERTS
NPAD_EXP = NB_EXP * TM_EXP
TF_EXP = 256
LANES = 128
D_ROW = D_MODEL + 3 * LANES

VMEM_LIMIT = 60 * 1024 * 1024

_BF = jnp.bfloat16
_F32 = jnp.float32


def _params(sem, vmem=VMEM_LIMIT):
    return pltpu.CompilerParams(dimension_semantics=sem, vmem_limit_bytes=vmem)


def _rms(x, g):
    return x * lax.rsqrt(jnp.mean(x * x, axis=-1, keepdims=True) + EPS) * g


def _window_sums(ext, h, rows, first, emit):
    for g, w in enumerate(POOL_WINDOWS):
        cols = slice(g * POOL_GROUP_DIM, (g + 1) * POOL_GROUP_DIM)
        acc = h(cols)
        for j in range(1, w):
            acc = acc + ext[pl.ds(first - j, rows), cols]
        emit(g, w, cols, acc)


def _pool_kernel(has_m, *refs):
    if has_m:
        x_ref, m_ref, hist_ref, g_ref, w_ref, sc_ref, o_ref, hp_ref, hs_ref, ext, wbf = refs
    else:
        x_ref, hist_ref, g_ref, w_ref, sc_ref, o_ref, hp_ref, hs_ref, ext, wbf = refs
        m_ref = None
    i = pl.program_id(0)

    @pl.when(i == 0)
    def _():
        ext[0:HIST_ROWS, :] = jnp.zeros((HIST_ROWS, D_MODEL), _F32)
        wbf[...] = w_ref[...].astype(_BF)

    def load_x():
        x = x_ref[...]
        if has_m:
            x = x + m_ref[...]
        return x

    @pl.when(i < N_SMALL_PROMPT)
    def _():
        x = load_x()
        h = _rms(x, g_ref[...])
        ext[HIST_ROWS:HIST_ROWS + TM_SMALL, :] = h
        pos = i * TM_SMALL + lax.broadcasted_iota(jnp.int32, (TM_SMALL, 1), 0)

        def emit(g, w, cols, acc):
            cnt = jnp.minimum(pos + 1, w).astype(_F32)
            pooled = acc / cnt - h[:, cols]
            y = jnp.dot(pooled.astype(_BF), wbf[g], preferred_element_type=_F32)
            o_ref[:, cols] = x[:, cols] + y * sc_ref[:, cols]

        _window_sums(ext, lambda cols: h[:, cols], TM_SMALL, HIST_ROWS, emit)
        tail = ext[TM_SMALL:TM_SMALL + HIST_ROWS, :]
        hp_ref[...] = tail
        ext[0:HIST_ROWS, :] = tail

    @pl.when(jnp.logical_and(i >= N_SMALL_PROMPT, i < N_SMALL_TOK))
    def _():
        x = load_x()
        h = _rms(x, g_ref[...])
        hs_ref[...] = h
        seg = 2 * HIST_ROWS
        for b in range(SEQ_PER_TILE):
            ext[b * seg:b * seg + HIST_ROWS, :] = hist_ref[b]
            ext[b * seg + HIST_ROWS:(b + 1) * seg, :] = h[b * DEC_SEQ:(b + 1) * DEC_SEQ, :]
        rows = SEQ_PER_TILE * seg - HIST_ROWS
        t = lax.broadcasted_iota(jnp.int32, (DEC_SEQ, 1), 0)

        def emit(g, w, cols, acc):
            cnt = jnp.minimum(PAST_LEN + t + 1, w).astype(_F32)
            pieces = [acc[b * seg:b * seg + DEC_SEQ, :] / cnt for b in range(SEQ_PER_TILE)]
            pooled = jnp.concatenate(pieces, axis=0) - h[:, cols]
            y = jnp.dot(pooled.astype(_BF), wbf[g], preferred_element_type=_F32)
            o_ref[:, cols] = x[:, cols] + y * sc_ref[:, cols]

        _window_sums(ext, lambda cols: ext[pl.ds(HIST_ROWS, rows), cols], rows, HIST_ROWS, emit)

    @pl.when(i >= N_SMALL_TOK)
    def _():
        o_ref[...] = jnp.zeros((TM_SMALL, D_MODEL), _F32)


def _pool_layer(x, m, hist16, g, w, sc):
    has_m = m is not None
    row = pl.BlockSpec((TM_SMALL, D_MODEL), lambda i: (i, 0))
    vec = pl.BlockSpec((1, D_MODEL), lambda i: (0, 0))
    samp = lambda i: (jnp.clip(i - N_SMALL_PROMPT, 0, N_SAMPLE // TM_SMALL - 1), 0)
    in_specs = [row] + ([row] if has_m else []) + [
        pl.BlockSpec((SEQ_PER_TILE, HIST_ROWS, D_MODEL), lambda i: (samp(i)[0], 0, 0)),
        vec,
        pl.BlockSpec((POOL_GROUPS, POOL_GROUP_DIM, POOL_GROUP_DIM), lambda i: (0, 0, 0)),
        vec,
    ]
    args = [x] + ([m] if has_m else []) + [hist16, g.reshape(1, -1), w, sc.reshape(1, -1)]
    return pl.pallas_call(
        functools.partial(_pool_kernel, has_m),
        grid=(N_SMALL,),
        in_specs=in_specs,
        out_specs=[
            row,
            pl.BlockSpec((HIST_ROWS, D_MODEL), lambda i: (0, 0)),
            pl.BlockSpec((TM_SMALL, D_MODEL), samp),
        ],
        out_shape=[
            jax.ShapeDtypeStruct((N_PAD, D_MODEL), _F32),
            jax.ShapeDtypeStruct((HIST_ROWS, D_MODEL), _F32),
            jax.ShapeDtypeStruct((N_SAMPLE, D_MODEL), _F32),
        ],
        scratch_shapes=[
            pltpu.VMEM((2 * HIST_ROWS * SEQ_PER_TILE, D_MODEL), _F32),
            pltpu.VMEM((POOL_GROUPS, POOL_GROUP_DIM, POOL_GROUP_DIM), _BF),
        ],
        compiler_params=_params(("arbitrary",)),
        name="pool_mixer",
    )(*args)


def _ffn_kernel(x_ref, g_ref, w1_ref, w3_ref, w2_ref, o_ref, hbf):
    f = pl.program_id(1)

    @pl.when(f == 0)
    def _():
        x = x_ref[...]
        hbf[...] = _rms(x, g_ref[...]).astype(_BF)
        o_ref[...] = x

    h = hbf[...]
    a = jnp.dot(h, w1_ref[...].astype(_BF), preferred_element_type=_F32)
    b = jnp.dot(h, w3_ref[...].astype(_BF), preferred_element_type=_F32)
    act = (a * jax.nn.sigmoid(a) * b).astype(_BF)
    o_ref[...] += jnp.dot(act, w2_ref[...].astype(_BF), preferred_element_type=_F32)


def _ffn_layer(layer, x, g, w1, w3, w2):
    return pl.pallas_call(
        _ffn_kernel,
        grid=(N_BIG, D_FF // TF_FFN),
        in_specs=[
            pl.BlockSpec((TM_BIG, D_MODEL), lambda i, f: (i, 0), pipeline_mode=pl.Buffered(1)),
            pl.BlockSpec((1, D_MODEL), lambda i, f: (0, 0)),
            pl.BlockSpec((None, D_MODEL, TF_FFN), lambda i, f: (layer, 0, f)),
            pl.BlockSpec((None, D_MODEL, TF_FFN), lambda i, f: (layer, 0, f)),
            pl.BlockSpec((None, TF_FFN, D_MODEL), lambda i, f: (layer, f, 0)),
        ],
        out_specs=pl.BlockSpec((TM_BIG, D_MODEL), lambda i, f: (i, 0), pipeline_mode=pl.Buffered(1)),
        out_shape=jax.ShapeDtypeStruct((N_PAD, D_MODEL), _F32),
        scratch_shapes=[pltpu.VMEM((TM_BIG, D_MODEL), _BF)],
        compiler_params=_params(("arbitrary", "arbitrary")),
        name="dense_swiglu",
    )(x, g.reshape(1, -1), w1, w3, w2)


def _gelu(z):
    return 0.5 * z * (1.0 + lax.erf(z * (2.0 ** -0.5)))


def _sgu_v_kernel(x_ref, g_ref, w_ref, vg_ref, vn_ref, vs_ref, hbf, vraw):
    i = pl.program_id(0)
    n = pl.program_id(1)

    @pl.when(n == 0)
    def _():
        hbf[...] = _rms(x_ref[...], g_ref[...]).astype(_BF)

    vraw[n] = _gelu(jnp.dot(hbf[...], w_ref[...].astype(_BF), preferred_element_type=_F32))

    @pl.when(n == SGU_HEADS - 1)
    def _():
        ss = jnp.zeros((TM_SGU2, 1), _F32)
        for k in range(SGU_HEADS):
            v = vraw[k]
            ss = ss + jnp.sum(v * v, axis=-1, keepdims=True)
        r = lax.rsqrt(ss * (1.0 / D_SGU) + EPS)
        for k in range(SGU_HEADS):
            cols = slice(k * SGU_HEAD_DIM, (k + 1) * SGU_HEAD_DIM)
            vn = vraw[k] * r * vg_ref[:, cols]
            vn_ref[:, cols] = vn.astype(_BF)

            @pl.when(i == SAMPLE_TILE_SGU2)
            def _():
                vs_ref[:, cols] = vn


def _sgu_v(layer, x, g, w_in, v_gain):
    return pl.pallas_call(
        _sgu_v_kernel,
        grid=(N_SGU2, SGU_HEADS),
        in_specs=[
            pl.BlockSpec((TM_SGU2, D_MODEL), lambda i, n: (i, 0)),
            pl.BlockSpec((1, D_MODEL), lambda i, n: (0, 0)),
            pl.BlockSpec((None, D_MODEL, SGU_HEAD_DIM), lambda i, n: (layer, 0, SGU_HEADS + n)),
            pl.BlockSpec((1, D_SGU), lambda i, n: (0, 0)),
        ],
        out_specs=[
            pl.BlockSpec((TM_SGU2, D_SGU), lambda i, n: (i, 0)),
            pl.BlockSpec((N_SAMPLE, D_SGU), lambda i, n: (0, 0)),
        ],
        out_shape=[
            jax.ShapeDtypeStruct((N_PAD, D_SGU), _BF),
            jax.ShapeDtypeStruct((N_SAMPLE, D_SGU), _F32),
        ],
        scratch_shapes=[
            pltpu.VMEM((TM_SGU2, D_MODEL), _BF),
            pltpu.VMEM((SGU_HEADS, TM_SGU2, SGU_HEAD_DIM), _F32),
        ],
        compiler_params=_params(("arbitrary", "arbitrary")),
        name="sgu_v",
    )(x, g.reshape(1, -1), w_in, v_gain.reshape(1, -1))


def _sgu_gate_kernel(x_ref, g_ref, wi_ref, vn_ref, wm_ref, bm_ref, wo_ref, o_ref, hbf):
    i = pl.program_id(0)
    g = pl.program_id(1)

    @pl.when(g == 0)
    def _():
        x = x_ref[...]
        hbf[...] = _rms(x, g_ref[...]).astype(_BF)
        o_ref[...] = x

    u = _gelu(jnp.dot(hbf[...], wi_ref[...].astype(_BF), preferred_element_type=_F32))
    r = lax.broadcasted_iota(jnp.int32, (SGU_CHUNK, SGU_CHUNK), 0)
    c = lax.broadcasted_iota(jnp.int32, (SGU_CHUNK, SGU_CHUNK), 1)
    seq_shift = DEC_SEQ.bit_length() - 1
    same_seq = jnp.right_shift(r, seq_shift) == jnp.right_shift(c, seq_shift)
    keep = jnp.logical_and(c <= r, jnp.logical_or(i != SAMPLE_TILE_BIG, same_seq))
    wm = jnp.where(keep, wm_ref[0, 0], 0.0).astype(_BF)
    bias = bm_ref[0, 0]
    parts = []
    for k in range(TM_BIG // SGU_CHUNK):
        rows = slice(k * SGU_CHUNK, (k + 1) * SGU_CHUNK)
        mixed = jnp.dot(wm, vn_ref[rows, :], preferred_element_type=_F32) + bias
        parts.append((u[rows, :] * mixed).astype(_BF))
    s = jnp.concatenate(parts, axis=0)
    o_ref[...] += jnp.dot(s, wo_ref[...].astype(_BF), preferred_element_type=_F32)


def _sgu_gate(layer, x, g, w_in, vn, wmix, bmix, w_out):
    ty = lambda i: jnp.where(i == SAMPLE_TILE_BIG, 1, 0)
    return pl.pallas_call(
        _sgu_gate_kernel,
        grid=(N_BIG, SGU_HEADS),
        in_specs=[
            pl.BlockSpec((TM_BIG, D_MODEL), lambda i, g: (i, 0), pipeline_mode=pl.Buffered(1)),
            pl.BlockSpec((1, D_MODEL), lambda i, g: (0, 0)),
            pl.BlockSpec((None, D_MODEL, SGU_HEAD_DIM), lambda i, g: (layer, 0, g)),
            pl.BlockSpec((TM_BIG, SGU_HEAD_DIM), lambda i, g: (i, g)),
            pl.BlockSpec((1, 1, SGU_CHUNK, SGU_CHUNK), lambda i, g: (ty(i), g, 0, 0)),
            pl.BlockSpec((1, 1, SGU_CHUNK, 1), lambda i, g: (ty(i), g, 0, 0)),
            pl.BlockSpec((None, SGU_HEAD_DIM, D_MODEL), lambda i, g: (layer, g, 0)),
        ],
        out_specs=pl.BlockSpec((TM_BIG, D_MODEL), lambda i, g: (i, 0)),
        out_shape=jax.ShapeDtypeStruct((N_PAD, D_MODEL), _F32),
        scratch_shapes=[pltpu.VMEM((TM_BIG, D_MODEL), _BF)],
        compiler_params=_params(("arbitrary", "arbitrary")),
        name="sgu_gate",
    )(x, g.reshape(1, -1), w_in, vn, wmix, bmix, w_out)


def _router_kernel(x_ref, g_ref, wr_ref, br_ref, h_ref, meta_ref, cnt_ref, carry):
    i = pl.program_id(0)

    @pl.when(i == 0)
    def _():
        carry[...] = jnp.zeros((1, N_EXPERTS), _F32)

    h = _rms(x_ref[...], g_ref[...])
    h_ref[:, 0:D_MODEL] = h
    logits =jnp.dot(h.astype(_BF), wr_ref[...].astype(_BF), preferred_element_type=_F32) + br_ref[...]
    idx = lax.broadcasted_iota(jnp.int32, (TM_SMALL, N_EXPERTS), 1).astype(_F32)
    none = float(N_EXPERTS)
    m1 = jnp.max(logits, axis=-1, keepdims=True)
    i1 = jnp.min(jnp.where(logits == m1, idx, none), axis=-1, keepdims=True)
    rest = jnp.where(idx == i1, -jnp.inf, logits)
    m2 = jnp.max(rest, axis=-1, keepdims=True)
    i2 = jnp.min(jnp.where(rest == m2, idx, none), axis=-1, keepdims=True)
    e2 = jnp.exp(m2 - m1)
    g1 = 1.0 / (1.0 + e2)
    g2 = e2 / (1.0 + e2)
    for k, col in enumerate([g1, g2, i1]):
        h_ref[:, D_MODEL + k * LANES:D_MODEL + (k + 1) * LANES] = jnp.broadcast_to(col, (TM_SMALL, LANES))

    row = i * TM_SMALL + lax.broadcasted_iota(jnp.int32, (TM_SMALL, 1), 0)
    chosen = jnp.logical_and(jnp.logical_or(idx == i1, idx == i2), row < N_TOK)
    onehot = chosen.astype(_F32)
    r = lax.broadcasted_iota(jnp.int32, (TM_SMALL, TM_SMALL), 0)
    c = lax.broadcasted_iota(jnp.int32, (TM_SMALL, TM_SMALL), 1)
    before = (c < r).astype(_BF)
    prefix = jnp.dot(before, onehot.astype(_BF), preferred_element_type=_F32) + carry[...]
    r1 = jnp.sum(jnp.where(idx == i1, prefix, 0.0), axis=-1, keepdims=True)
    r2 = jnp.sum(jnp.where(idx == i2, prefix, 0.0), axis=-1, keepdims=True)
    carry[...] += jnp.sum(onehot, axis=0, keepdims=True)
    cnt_ref[...] = carry[...]

    meta = jnp.zeros((TM_SMALL, N_EXPERTS), _F32)
    for k, col in enumerate([i1, i2, g1, g2, r1, r2]):
        meta = jnp.where(idx == float(k), col, meta)
    meta_ref[...] = meta


def _router(x, g, w_router, b_router):
    return pl.pallas_call(
        _router_kernel,
        grid=(N_SMALL,),
        in_specs=[
            pl.BlockSpec((TM_SMALL, D_MODEL), lambda i: (i, 0)),
            pl.BlockSpec((1, D_MODEL), lambda i: (0, 0)),
            pl.BlockSpec((D_MODEL, N_EXPERTS), lambda i: (0, 0)),
            pl.BlockSpec((1, N_EXPERTS), lambda i: (0, 0)),
        ],
        out_specs=[
            pl.BlockSpec((TM_SMALL, D_ROW), lambda i: (i, 0)),
            pl.BlockSpec((TM_SMALL, N_EXPERTS), lambda i: (i, 0)),
            pl.BlockSpec((1, N_EXPERTS), lambda i: (0, 0)),
        ],
        out_shape=[
            jax.ShapeDtypeStruct((N_PAD, D_ROW), _F32),
            jax.ShapeDtypeStruct((N_PAD, N_EXPERTS), _F32),
            jax.ShapeDtypeStruct((1, N_EXPERTS), _F32),
        ],
        scratch_shapes=[pltpu.VMEM((1, N_EXPERTS), _F32)],
        compiler_params=_params(("arbitrary",)),
        name="moe_router",
    )(x, g.reshape(1, -1), w_router, b_router.reshape(1, -1))


def _row_copy(src, dst, sem):
    return pltpu.make_async_copy(src, dst, sem)


def _scatter_kernel(d1_ref, d2_ref, h_ref, xs_in, xs_ref, sem):
    del xs_in

    def start(t, carry):
        _row_copy(h_ref.at[t], xs_ref.at[d1_ref[t]], sem).start()
        _row_copy(h_ref.at[t], xs_ref.at[d2_ref[t]], sem).start()
        return carry

    lax.fori_loop(0, TM_SMALL, start, 0)

    def wait(t, carry):
        _row_copy(h_ref.at[0], xs_ref.at[0], sem).wait()
        _row_copy(h_ref.at[0], xs_ref.at[0], sem).wait()
        return carry

    lax.fori_loop(0, TM_SMALL, wait, 0)


def _scatter_rows(d1, d2, h3):
    any_spec = pl.BlockSpec(memory_space=pl.ANY)
    smem = pl.BlockSpec((TM_SMALL,), lambda i: (i,), memory_space=pltpu.SMEM)
    xs0 = jnp.zeros((NPAD_EXP, 1, D_ROW), _F32)
    return pl.pallas_call(
        _scatter_kernel,
        grid=(N_SMALL_TOK,),
        in_specs=[smem, smem, pl.BlockSpec((TM_SMALL, 1, D_ROW), lambda i: (i, 0, 0)), any_spec],
        out_specs=any_spec,
        out_shape=jax.ShapeDtypeStruct((NPAD_EXP, 1, D_ROW), _F32),
        input_output_aliases={3: 0},
        scratch_shapes=[pltpu.SemaphoreType.DMA(())],
        compiler_params=pltpu.CompilerParams(
            dimension_semantics=("arbitrary",), has_side_effects=True),
        name="moe_scatter",
    )(d1, d2, h3, xs0)


def _expert_kernel(layer, be_ref, nv_ref, x_ref, w1_ref, w3_ref, w2_ref, o_ref, xbf, gate):
    del layer
    b = pl.program_id(0)
    f = pl.program_id(1)
    valid = b < nv_ref[0]

    @pl.when(f == 0)
    def _():
        xbf[...] = x_ref[:, 0:D_MODEL].astype(_BF)
        g1 = x_ref[:, D_MODEL:D_MODEL + 1]
        g2 = x_ref[:, D_MODEL + LANES:D_MODEL + LANES + 1]
        e1 = x_ref[:, D_MODEL + 2 * LANES:D_MODEL + 2 * LANES + 1]
        gate[...] = jnp.where(e1 == be_ref[b].astype(_F32), g1, g2)
        o_ref[...] = jnp.zeros((TM_EXP, D_MODEL), _F32)

    @pl.when(valid)
    def _():
        x = xbf[...]
        a = jnp.dot(x, w1_ref[...].astype(_BF), preferred_element_type=_F32)
        c = jnp.dot(x, w3_ref[...].astype(_BF), preferred_element_type=_F32)
        act = (a * jax.nn.sigmoid(a) * c).astype(_BF)
        o_ref[...] += jnp.dot(act, w2_ref[...].astype(_BF), preferred_element_type=_F32)

    @pl.when(f == pl.num_programs(1) - 1)
    def _():
        o_ref[...] = o_ref[...] * gate[...]


def _experts(layer, block_expert, n_valid, xs, w1, w3, w2):
    nf = D_EXP // TF_EXP

    def f_eff(b, f, nv):
        return jnp.where(b < nv[0], f, nf - 1)

    grid_spec = pltpu.PrefetchScalarGridSpec(
        num_scalar_prefetch=2,
        grid=(NB_EXP, nf),
        in_specs=[
            pl.BlockSpec((TM_EXP, D_ROW), lambda b, f, be, nv: (b, 0), pipeline_mode=pl.Buffered(1)),
            pl.BlockSpec((None, None, D_MODEL, TF_EXP),
                         lambda b, f, be, nv: (layer, be[b], 0, f_eff(b, f, nv))),
            pl.BlockSpec((None, None, D_MODEL, TF_EXP),
                         lambda b, f, be, nv: (layer, be[b], 0, f_eff(b, f, nv))),
            pl.BlockSpec((None, None, TF_EXP, D_MODEL),
                         lambda b, f, be, nv: (layer, be[b], f_eff(b, f, nv), 0)),
        ],
        out_specs=pl.BlockSpec((TM_EXP, D_MODEL), lambda b, f, be, nv: (b, 0)),
        scratch_shapes=[pltpu.VMEM((TM_EXP, D_MODEL), _BF), pltpu.VMEM((TM_EXP, 1), _F32)],
    )
    return pl.pallas_call(
        functools.partial(_expert_kernel, layer),
        grid_spec=grid_spec,
        out_shape=jax.ShapeDtypeStruct((NPAD_EXP, D_MODEL), _F32),
        compiler_params=_params(("arbitrary", "arbitrary")),
        name="moe_experts",
    )(block_expert, n_valid, xs, w1, w3, w2)


def _combine_kernel(d1_ref, d2_ref, out_ref, m_ref, buf_a, buf_b, sem):
    i = pl.program_id(0)

    @pl.when(i < N_SMALL_TOK)
    def _():
        def start(t, carry):
            _row_copy(out_ref.at[d1_ref[t]], buf_a.at[t], sem).start()
            _row_copy(out_ref.at[d2_ref[t]], buf_b.at[t], sem).start()
            return carry

        lax.fori_loop(0, TM_SMALL, start, 0)

        def wait(t, carry):
            _row_copy(out_ref.at[0], buf_a.at[0], sem).wait()
            _row_copy(out_ref.at[0], buf_b.at[0], sem).wait()
            return carry

        lax.fori_loop(0, TM_SMALL, wait, 0)
        m_ref[...] = buf_a[...] + buf_b[...]

    @pl.when(i >= N_SMALL_TOK)
    def _():
        m_ref[...] = jnp.zeros((TM_SMALL, 1, D_MODEL), _F32)


def _combine(d1, d2, out3):
    smem = pl.BlockSpec((TM_SMALL,), lambda i: (i,), memory_space=pltpu.SMEM)
    return pl.pallas_call(
        _combine_kernel,
        grid=(N_SMALL,),
        in_specs=[smem, smem, pl.BlockSpec(memory_space=pl.ANY)],
        out_specs=pl.BlockSpec((TM_SMALL, 1, D_MODEL), lambda i: (i, 0, 0)),
        out_shape=jax.ShapeDtypeStruct((N_PAD, 1, D_MODEL), _F32),
        scratch_shapes=[
            pltpu.VMEM((TM_SMALL, 1, D_MODEL), _F32),
            pltpu.VMEM((TM_SMALL, 1, D_MODEL), _F32),
            pltpu.SemaphoreType.DMA(()),
        ],
        compiler_params=_params(("arbitrary",)),
        name="moe_combine",
    )(d1, d2, out3)


def _moe_layer(layer, x, g, w_router, b_router, w1, w3, w2):
    hx, meta, counts = _router(x, g, w_router, b_router)
    counts = counts[0].astype(jnp.int32)
    padded = (counts + TM_EXP - 1) // TM_EXP * TM_EXP
    pad_end = jnp.cumsum(padded)
    pad_start = pad_end - padded
    e1 = meta[:, 0].astype(jnp.int32)
    e2 = meta[:, 1].astype(jnp.int32)
    d1 = pad_start[e1] + meta[:, 4].astype(jnp.int32)
    d2 = pad_start[e2] + meta[:, 5].astype(jnp.int32)
    n_valid = pad_end[-1:] // TM_EXP
    blk = jnp.arange(NB_EXP, dtype=jnp.int32)
    block_expert = jnp.searchsorted(pad_end, jnp.minimum(blk, n_valid[0] - 1) * TM_EXP, side='right')
    block_expert = jnp.minimum(block_expert, N_EXPERTS - 1).astype(jnp.int32)
    xs3 = _scatter_rows(d1, d2, hx.reshape(N_PAD, 1, D_ROW))
    out = _experts(layer, block_expert, n_valid.astype(jnp.int32), xs3.reshape(NPAD_EXP, D_ROW), w1, w3, w2)
    m3 = _combine(d1, d2, out.reshape(NPAD_EXP, 1, D_MODEL))
    return m3.reshape(N_PAD, D_MODEL)


def _final_kernel(x_ref, m_ref, g_ref, yp_ref, ys_ref):
    i = pl.program_id(0)
    y = _rms(x_ref[...] + m_ref[...], g_ref[...])

    @pl.when(i < N_SMALL_PROMPT)
    def _():
        yp_ref[...] = y

    @pl.when(jnp.logical_and(i >= N_SMALL_PROMPT, i < N_SMALL_TOK))
    def _():
        ys_ref[...] = y


def _final_norm(x, m, g):
    row = pl.BlockSpec((TM_SMALL, D_MODEL), lambda i: (i, 0))
    return pl.pallas_call(
        _final_kernel,
        grid=(N_SMALL_TOK,),
        in_specs=[row, row, pl.BlockSpec((1, D_MODEL), lambda i: (0, 0))],
        out_specs=[
            pl.BlockSpec((TM_SMALL, D_MODEL), lambda i: (jnp.minimum(i, N_SMALL_PROMPT - 1), 0)),
            pl.BlockSpec((TM_SMALL, D_MODEL),
                         lambda i: (jnp.clip(i - N_SMALL_PROMPT, 0, N_SAMPLE // TM_SMALL - 1), 0)),
        ],
        out_shape=[
            jax.ShapeDtypeStruct((SEQ, D_MODEL), _F32),
            jax.ShapeDtypeStruct((N_SAMPLE, D_MODEL), _F32),
        ],
        compiler_params=_params(("arbitrary",)),
        name="final_norm",
    )(x, m, g.reshape(1, -1))


def _prepare(x_prompt, x_sample, state_pool, sgu_w_s, sgu_b_s):
    x = jnp.concatenate([
        x_prompt.reshape(SEQ, D_MODEL),
        x_sample.reshape(N_SAMPLE, D_MODEL),
        jnp.zeros((N_PAD - N_TOK, D_MODEL), _F32),
    ], axis=0)
    hist16 = jnp.pad(state_pool, ((0, 0), (0, 0), (1, 0), (0, 0)))
    reps = SGU_CHUNK // DEC_SEQ
    wmix = jnp.stack([sgu_w_s, jnp.tile(sgu_w_s[:, :, :DEC_SEQ, :DEC_SEQ], (1, 1, reps, reps))], axis=1)
    bmix = jnp.stack([sgu_b_s, jnp.tile(sgu_b_s[:, :, :DEC_SEQ], (1, 1, reps))], axis=1)[..., None]
    return x, hist16, wmix, bmix


def _sgu_layer(layer, x, g, w_in, v_gain, wmix, bmix, w_out):
    vn, v_sample = _sgu_v(layer, x, g, w_in, v_gain)
    x = _sgu_gate(layer, x, g, w_in, vn, wmix, bmix, w_out)
    return x, v_sample.reshape(DEC_BATCH, DEC_SEQ, D_SGU)


def kernel(x_prompt, x_sample, state_pool, norm_mix, norm_ffn, norm_final, pool_w, pool_scale,
           sgu_w_in, sgu_v_gain, sgu_w_s, sgu_b_s, sgu_w_out, ffn_w1, ffn_w3, ffn_w2,
           moe_router, moe_router_b, moe_w1, moe_w3, moe_w2):
    x, hist16, wmix, bmix = _prepare(x_prompt, x_sample, state_pool, sgu_w_s, sgu_b_s)

    m = None
    pool_p, pool_s, sgu_v = [], [], []
    for i in range(DEPTH):
        j = i // 2
        if i % 2 == 0:
            x, hp, hs = _pool_layer(x, m, hist16[j], norm_mix[i], pool_w[j], pool_scale[j])
            pool_p.append(hp[1:])
            pool_s.append(hs.reshape(DEC_BATCH, DEC_SEQ, D_MODEL)[:, 1:])
            x = _ffn_layer(j, x, norm_ffn[i], ffn_w1, ffn_w3, ffn_w2)
        else:
            x, v_sample = _sgu_layer(j, x, norm_mix[i], sgu_w_in, sgu_v_gain[j], wmix[j], bmix[j], sgu_w_out)
            sgu_v.append(v_sample)
            m = _moe_layer(j, x, norm_ffn[i], moe_router[j], moe_router_b[j], moe_w1, moe_w3, moe_w2)
    y_prompt, y_sample = _final_norm(x, m, norm_final)
    return (
        y_prompt.reshape(1, SEQ, D_MODEL),
        y_sample.reshape(DEC_BATCH, DEC_SEQ, D_MODEL),
        jnp.stack(pool_p)[:, None],
        jnp.stack(pool_s),
        jnp.stack(sgu_v),
    )
```
